```python
import jax, jax.numpy as jnp
from jax import lax
import numpy as np

D_MODEL = 1024
BATCH = 2
SEQ = 8192
DEPTH = 1

HEAD_DIM = 64
MOBA_HEADS = 8
FOX_HEADS = 8
A_WIDTH = MOBA_HEADS * HEAD_DIM
B_WIDTH = FOX_HEADS * HEAD_DIM
ROPE_DIM = HEAD_DIM // 4
ROPE_THETA = 500000.0
MOBA_BLOCK = 256
MOBA_TOPK = 3
MOBA_Q_CHUNK = 64
Q_BLOCK = 128
FORGET_BIAS_CENTER = 3.0
PEER_HEADS = 8
PEER_NKEYS = 128
PEER_NEXPERTS = PEER_NKEYS * PEER_NKEYS
PEER_KEY_DIM = 256
PEER_HALF = PEER_KEY_DIM // 2
PEER_TOPK = 16
PEER_TOKEN_CHUNK = 128
IN_COLS = 3 * A_WIDTH + 3 * B_WIDTH + FOX_HEADS + 2 * D_MODEL
RMS_EPS = 1e-6

kernel_name = "moba_fox_gated_peer_hybrid"


def rms_norm(x, g):
    xf = x.astype(jnp.float32)
    y = xf * lax.rsqrt(jnp.mean(xf * xf, axis=-1, keepdims=True) + RMS_EPS)
    return (y * g.astype(jnp.float32)).astype(x.dtype)


def partial_rope(x, pos):
    half = ROPE_DIM // 2
    inv_freq = jnp.power(ROPE_THETA, -jnp.arange(half, dtype=jnp.float32) / half)
    ang = pos.astype(jnp.float32)[:, None] * inv_freq[None, :]
    cos = jnp.cos(ang).astype(x.dtype)
    sin = jnp.sin(ang).astype(x.dtype)
    x1 = x[..., :half]
    x2 = x[..., half:ROPE_DIM]
    return jnp.concatenate([x1 * cos - x2 * sin, x1 * sin + x2 * cos, x[..., ROPE_DIM:]], axis=-1)


def moba_attention(q, k, v):
    B, H, S, dh = q.shape
    L = MOBA_BLOCK
    nb = -(-S // L)
    pad = nb * L - S
    k_blk = jnp.pad(k, ((0, 0), (0, 0), (0, pad), (0, 0))).reshape(B, H, nb, L, dh)
    v_blk = jnp.pad(v, ((0, 0), (0, 0), (0, pad), (0, 0))).reshape(B, H, nb, L, dh)
    k_mean = jnp.mean(k_blk.astype(jnp.float32), axis=3).astype(k.dtype)
    scale = dh ** -0.5
    qpos = jnp.arange(S)
    gate = jnp.einsum('bhsd,bhnd->bhsn', q, k_mean).astype(jnp.float32)
    past = jnp.arange(nb)[None, :] < (qpos // L)[:, None]
    gate = jnp.where(past, gate, -jnp.inf)
    kk = min(MOBA_TOPK, nb)
    top_val, top_idx = lax.top_k(gate, kk)
    sel_valid = jnp.isfinite(top_val)
    C = MOBA_Q_CHUNK
    nq = S // C
    q_c = q.reshape(B, H, nq, C, dh).transpose(2, 0, 1, 3, 4)
    idx_c = top_idx.reshape(B, H, nq, C, kk).transpose(2, 0, 1, 3, 4)
    val_c = sel_valid.reshape(B, H, nq, C, kk).transpose(2, 0, 1, 3, 4)
    take_blocks = jax.vmap(jax.vmap(lambda tab, ix: tab[ix]))

    def step(args):
        ci, qi, ii, vi = args
        q0 = ci * C
        own = q0 // L
        ks = take_blocks(k_blk, ii)
        vs = take_blocks(v_blk, ii)
        s_sel = jnp.einsum('bhcd,bhcjld->bhcjl', qi, ks).astype(jnp.float32) * scale
        s_sel = jnp.where(vi[..., None], s_sel, -jnp.inf).reshape(B, H, C, kk * L)
        k_own = lax.dynamic_index_in_dim(k_blk, own, axis=2, keepdims=False)
        v_own = lax.dynamic_index_in_dim(v_blk, own, axis=2, keepdims=False)
        s_own = jnp.einsum('bhcd,bhld->bhcl', qi, k_own).astype(jnp.float32) * scale
        causal = (own * L + jnp.arange(L))[None, :] <= (q0 + jnp.arange(C))[:, None]
        s_own = jnp.where(causal, s_own, -jnp.inf)
        p = jax.nn.softmax(jnp.concatenate([s_sel, s_own], axis=-1), axis=-1)
        p_sel = p[..., :kk * L].reshape(B, H, C, kk, L).astype(v.dtype)
        p_own = p[..., kk * L:].astype(v.dtype)
        return (jnp.einsum('bhcjl,bhcjld->bhcd', p_sel, vs)
                + jnp.einsum('bhcl,bhld->bhcd', p_own, v_own))

    out = lax.map(step, (jnp.arange(nq), q_c, idx_c, val_c))
    return out.transpose(1, 2, 0, 3, 4).reshape(B, H, S, dh)


def forgetting_attention(q, k, v, log_f):
    B, H, S, dh = q.shape
    scale = dh ** -0.5
    c = jnp.cumsum(log_f, axis=-1)
    nq = S // Q_BLOCK
    q_c = q.reshape(B, H, nq, Q_BLOCK, dh).transpose(2, 0, 1, 3, 4)
    c_c = c.reshape(B, H, nq, Q_BLOCK).transpose(2, 0, 1, 3)
    kpos = jnp.arange(S)

    def step(args):
        ci, qi, cq = args
        s = (jnp.einsum('bhqd,bhkd->bhqk', qi, k).astype(jnp.float32) * scale
             + cq[..., None] - c[:, :, None, :])
        qpos = ci * Q_BLOCK + jnp.arange(Q_BLOCK)
        s = jnp.where(kpos[None, :] <= qpos[:, None], s, -jnp.inf)
        p = jax.nn.softmax(s, axis=-1).astype(v.dtype)
        return jnp.einsum('bhqk,bhkd->bhqd', p, v)

    out = lax.map(step, (jnp.arange(nq), q_c, c_c))
    return out.transpose(1, 2, 0, 3, 4).reshape(B, H, S, dh)


def hybrid_mixer(xn, w_in, b_forget, w_branch_a, w_branch_b, w_out):
    B, S, _ = xn.shape
    proj = xn @ w_in
    cuts = [A_WIDTH, 2 * A_WIDTH, 3 * A_WIDTH,
            3 * A_WIDTH + B_WIDTH, 3 * A_WIDTH + 2 * B_WIDTH, 3 * A_WIDTH + 3 * B_WIDTH,
            3 * A_WIDTH + 3 * B_WIDTH + FOX_HEADS, 3 * A_WIDTH + 3 * B_WIDTH + FOX_HEADS + D_MODEL]
    qa, ka, va, qb, kb, vb, f_logit, g_a, g_b = jnp.split(proj, cuts, axis=-1)

    def heads(t, n):
        return t.reshape(B, S, n, HEAD_DIM).transpose(0, 2, 1, 3)

    pos = jnp.arange(S)
    ya = moba_attention(partial_rope(heads(qa, MOBA_HEADS), pos),
                        partial_rope(heads(ka, MOBA_HEADS), pos),
                        heads(va, MOBA_HEADS))
    ya = ya.transpose(0, 2, 1, 3).reshape(B, S, A_WIDTH) @ w_branch_a
    log_f = jax.nn.log_sigmoid((f_logit + b_forget).astype(jnp.float32)).transpose(0, 2, 1)
    yb = forgetting_attention(heads(qb, FOX_HEADS), heads(kb, FOX_HEADS), heads(vb, FOX_HEADS), log_f)
    yb = yb.transpose(0, 2, 1, 3).reshape(B, S, B_WIDTH) @ w_branch_b
    merged = jax.nn.sigmoid(g_a) * ya + jax.nn.sigmoid(g_b) * yb
    return merged @ w_out


def peer_ffn(xn, w_peer_q, sub_keys, expert_u, expert_v):
    B, S, D = xn.shape
    T = B * S
    xt = xn.reshape(T, D)
    qh = (xt @ w_peer_q).reshape(T, PEER_HEADS, 2, PEER_HALF)
    s = jnp.einsum('thpd,hpnd->thpn', qh, sub_keys).astype(jnp.float32)
    sv, si = lax.top_k(s, PEER_TOPK)
    cand = (sv[:, :, 0, :, None] + sv[:, :, 1, None, :]).reshape(T, PEER_HEADS, PEER_TOPK * PEER_TOPK)
    cand_idx = (si[:, :, 0, :, None] * PEER_NKEYS + si[:, :, 1, None, :]).reshape(T, PEER_HEADS, PEER_TOPK * PEER_TOPK)
    best, pick = lax.top_k(cand, PEER_TOPK)
    expert_idx = jnp.take_along_axis(cand_idx, pick, axis=-1)
    gates = jax.nn.softmax(best, axis=-1)
    C = PEER_TOKEN_CHUNK
    n_chunks = T // C

    def step(args):
        xc, ic, gc = args
        u = expert_u[ic]
        hval = jnp.einsum('cd,chkd->chk', xc, u).astype(jnp.float32)
        a = (gc * jax.nn.gelu(hval, approximate=False)).astype(xc.dtype)
        return jnp.einsum('chk,chkd->cd', a, expert_v[ic])

    out = lax.map(step, (xt.reshape(n_chunks, C, D),
                         expert_idx.reshape(n_chunks, C, PEER_HEADS, PEER_TOPK),
                         gates.reshape(n_chunks, C, PEER_HEADS, PEER_TOPK)))
    return out.reshape(B, S, D)


def setup_inputs(seed: int = 0) -> dict:
    key = jax.random.key(seed)
    ks = jax.random.split(key, 14)
    f32 = jnp.float32

    def nrm(k, shape, scale):
        return jax.random.normal(k, shape, f32) * scale

    return {
        "x": nrm(ks[0], (BATCH, SEQ, D_MODEL), 1.0),
        "norm_mix_g": 1.0 + nrm(ks[1], (DEPTH, D_MODEL), 0.02),
        "w_in": nrm(ks[2], (DEPTH, D_MODEL, IN_COLS), D_MODEL ** -0.5),
        "b_forget": FORGET_BIAS_CENTER + nrm(ks[3], (DEPTH, FOX_HEADS), 0.5),
        "w_branch_a": nrm(ks[4], (DEPTH, A_WIDTH, D_MODEL), A_WIDTH ** -0.5),
        "w_branch_b": nrm(ks[5], (DEPTH, B_WIDTH, D_MODEL), B_WIDTH ** -0.5),
        "w_out": nrm(ks[6], (DEPTH, D_MODEL, D_MODEL), D_MODEL ** -0.5),
        "norm_ffn_g": 1.0 + nrm(ks[7], (DEPTH, D_MODEL), 0.02),
        "w_peer_q": nrm(ks[8], (DEPTH, D_MODEL, PEER_HEADS * PEER_KEY_DIM), D_MODEL ** -0.5),
        "peer_sub_keys": nrm(ks[9], (DEPTH, PEER_HEADS, 2, PEER_NKEYS, PEER_HALF), PEER_HALF ** -0.5),
        "peer_expert_u": nrm(ks[10], (DEPTH, PEER_NEXPERTS, D_MODEL), D_MODEL ** -0.5),
        "peer_expert_v": nrm(ks[11], (DEPTH, PEER_NEXPERTS, D_MODEL), PEER_HEADS ** -0.5),
        "norm_final_g": 1.0 + nrm(ks[12], (D_MODEL,), 0.02),
    }


def reference(x, norm_mix_g, w_in, b_forget, w_branch_a, w_branch_b, w_out, norm_ffn_g,
              w_peer_q, peer_sub_keys, peer_expert_u, peer_expert_v, norm_final_g):
    h = x
    for l in range(DEPTH):
        h = h + hybrid_mixer(rms_norm(h, norm_mix_g[l]), w_in[l], b_forget[l],
                             w_branch_a[l], w_branch_b[l], w_out[l])
        h = h + peer_ffn(rms_norm(h, norm_ffn_g[l]), w_peer_q[l], peer_sub_keys[l],
                         peer_expert_u[l], peer_expert_v[l])
    return rms_norm(h, norm_final_g)
```

```python
import functools

import numpy as np
import jax
import jax.numpy as jnp
from jax import lax
from jax.experimental import pallas as pl
from jax.experimental.pallas import tpu as pltpu

F32 = jnp.float32
BF16 = jnp.bfloat16

LANES = 128

HEAD_DIM = 64
HEADS = 8
WIDTH = HEADS * HEAD_DIM
PAIRS = WIDTH // LANES
ROPE_DIM = HEAD_DIM // 4
ROPE_HALF = ROPE_DIM // 2
ROPE_THETA = 500000.0
MOBA_BLOCK = 256
MOBA_TOPK = 3
RMS_EPS = 1e-6
PEER_HEADS = 8
PEER_NKEYS = 128
PEER_HALF = 128
PEER_TOPK = 16
ATTN_SCALE = HEAD_DIM ** -0.5

MASK_BIG = 2.0 ** 100
SQRT_HALF = 0.5 ** 0.5
NEG_INIT = -1e30

VMEM_LIMIT = 56 * 1024 * 1024


def _params(sem):
    return pltpu.CompilerParams(dimension_semantics=sem, vmem_limit_bytes=VMEM_LIMIT)


def _split3(v):
    hi = v.astype(BF16)
    r = v - hi.astype(F32)
    mid = r.astype(BF16)
    lo = (r - mid.astype(F32)).astype(BF16)
    return hi, mid, lo


def _inproj_kernel(x_ref, g_ref, wqkv_ref, wg_ref, wf_ref, bf_ref, cos_ref, sinp_ref, sinm_ref,
                   tri_ref, place_ref, qkv_ref, gate_ref, kmean_ref, caug_ref, carry_ref,
                   *, tiles_per_seq):
    i = pl.program_id(0)
    tm = x_ref.shape[0]
    x = x_ref[...]
    xn = x * lax.rsqrt(jnp.mean(x * x, axis=-1, keepdims=True) + RMS_EPS) * g_ref[...]
    xb = xn.astype(BF16)

    qkv = jnp.dot(xb, wqkv_ref[...], preferred_element_type=F32)
    cos, sinp, sinm = cos_ref[...], sinp_ref[...], sinm_ref[...]

    def rope(c0):
        t = qkv[:, c0:c0 + LANES]
        return (t * cos + pltpu.roll(t, ROPE_HALF, 1) * sinp
                + pltpu.roll(t, LANES - ROPE_HALF, 1) * sinm)

    ksum = []
    for c in range(PAIRS):
        qa = rope(c * LANES) * ATTN_SCALE
        ka = rope(WIDTH + c * LANES)
        qkv_ref[:, c * LANES:(c + 1) * LANES] = qa.astype(BF16)
        qkv_ref[:, WIDTH + c * LANES:WIDTH + (c + 1) * LANES] = ka.astype(BF16)
        ksum.append(jnp.mean(ka, axis=0, keepdims=True))
    kmean_ref[...] = jnp.concatenate(ksum, axis=1)
    qkv_ref[:, 2 * WIDTH:3 * WIDTH] = qkv[:, 2 * WIDTH:3 * WIDTH].astype(BF16)
    qkv_ref[:, 3 * WIDTH:4 * WIDTH] = (qkv[:, 3 * WIDTH:4 * WIDTH] * ATTN_SCALE).astype(BF16)
    qkv_ref[:, 4 * WIDTH:6 * WIDTH] = qkv[:, 4 * WIDTH:6 * WIDTH].astype(BF16)

    gate_ref[...] = jax.nn.sigmoid(jnp.dot(xb, wg_ref[...], preferred_element_type=F32))

    fl = jnp.dot(xb, wf_ref[...], preferred_element_type=F32) + bf_ref[...]
    logf = jnp.minimum(fl, 0.0) - jnp.log1p(jnp.exp(-jnp.abs(fl)))
    tri = tri_ref[...]
    cum = sum(jnp.dot(tri, part, preferred_element_type=F32) for part in _split3(logf))

    @pl.when(i % tiles_per_seq == 0)
    def _():
        carry_ref[...] = jnp.zeros_like(carry_ref)

    c = cum + carry_ref[...]
    carry_ref[...] = c[tm - 1:tm, :]
    aug = sum(jnp.dot(part, place_ref[j], preferred_element_type=F32)
              for j, part in enumerate(_split3(-c)))
    caug_ref[...] = aug.astype(BF16)


def _rope_tables(S):
    half = ROPE_HALF
    inv_freq = jnp.power(ROPE_THETA, -jnp.arange(half, dtype=F32) / half)
    ang = jnp.arange(S).astype(F32)[:, None] * inv_freq[None, :]
    cos, sin = jnp.cos(ang), jnp.sin(ang)
    ones = jnp.ones((S, HEAD_DIM - ROPE_DIM), F32)
    zeros = jnp.zeros((S, HEAD_DIM - ROPE_DIM), F32)
    zh = jnp.zeros((S, half), F32)
    cos_h = jnp.concatenate([cos, cos, ones], axis=1)
    sinp_h = jnp.concatenate([zh, sin, zeros], axis=1)
    sinm_h = jnp.concatenate([-sin, zh, zeros], axis=1)
    rep = LANES // HEAD_DIM
    return tuple(jnp.tile(t, (1, rep)) for t in (cos_h, sinp_h, sinm_h))


def _place_matrices():
    p = np.zeros((3, LANES, WIDTH), np.float32)
    for h in range(HEADS):
        for j in range(3):
            p[j, h, (h // 2) * LANES + 3 * (h % 2) + j] = 1.0
    return jnp.asarray(p, BF16)


def _inproj(xt, g, w_in, b_forget, S):
    T, D = xt.shape
    tm = MOBA_BLOCK
    n_qkv = 6 * WIDTH
    wqkv = w_in[:, :n_qkv].astype(BF16)
    wf = jnp.pad(w_in[:, n_qkv:n_qkv + HEADS], ((0, 0), (0, LANES - HEADS))).astype(BF16)
    wg = w_in[:, n_qkv + HEADS:].astype(BF16)
    bf = jnp.pad(b_forget, (0, LANES - HEADS)).reshape(1, LANES).astype(F32)
    cos, sinp, sinm = _rope_tables(S)
    tri = jnp.asarray(np.tril(np.ones((tm, tm), np.float32)), BF16)
    place = _place_matrices()
    tiles_per_seq = S // tm
    const = lambda *shape: pl.BlockSpec(shape, lambda i: (0,) * len(shape))
    pos = pl.BlockSpec((tm, LANES), lambda i: (i % tiles_per_seq, 0))
    return pl.pallas_call(
        functools.partial(_inproj_kernel, tiles_per_seq=tiles_per_seq),
        grid=(T // tm,),
        in_specs=[pl.BlockSpec((tm, D), lambda i: (i, 0)), const(1, D), const(D, n_qkv),
                  const(D, 2 * D), const(D, LANES), const(1, LANES), pos, pos, pos,
                  const(tm, tm), const(3, LANES, WIDTH)],
        out_specs=[pl.BlockSpec((tm, n_qkv), lambda i: (i, 0)),
                   pl.BlockSpec((tm, 2 * D), lambda i: (i, 0)),
                   pl.BlockSpec((None, 1, WIDTH), lambda i: (i, 0, 0)),
                   pl.BlockSpec((tm, WIDTH), lambda i: (i, 0))],
        out_shape=[jax.ShapeDtypeStruct((T, n_qkv), BF16),
                   jax.ShapeDtypeStruct((T, 2 * D), F32),
                   jax.ShapeDtypeStruct((T // tm, 1, WIDTH), F32),
                   jax.ShapeDtypeStruct((T, WIDTH), BF16)],
        scratch_shapes=[pltpu.VMEM((1, LANES), F32)],
        compiler_params=_params(("arbitrary",)),
        name="inproj",
    )(xt, g.reshape(1, D), wqkv, wg, wf, bf, cos, sinp, sinm, tri, place)


def _attn_kernel(*refs, moba, tq):
    if moba:
        q_ref, k_ref, v_ref, aug_ref, kmean_ref, o_ref, lhs_ref, m_ref, l_ref, acc_ref = refs
    else:
        q_ref, k_ref, v_ref, aug_ref, o_ref, lhs_ref, m_ref, l_ref, acc_ref = refs
    qi = pl.program_id(2)
    tk = tq
    rows = 2 * tq
    q = q_ref[...]
    lane_q = lax.broadcasted_iota(jnp.int32, (tq, LANES), 1)
    zq = jnp.zeros_like(q)
    qs = jnp.concatenate([jnp.where(lane_q < HEAD_DIM, q, zq),
                          jnp.where(lane_q < HEAD_DIM, zq, q)], axis=0)
    row = lax.broadcasted_iota(jnp.int32, (rows, LANES), 0)
    lane = lax.broadcasted_iota(jnp.int32, (rows, LANES), 1)

    if moba:
        nb = kmean_ref.shape[0]
        km = jnp.concatenate([kmean_ref[...].astype(BF16),
                              jnp.zeros((LANES - nb, LANES), BF16)], axis=0)
        gate = lax.dot_general(qs, km, (((1,), (1,)), ((), ())), preferred_element_type=F32)
        gate = jnp.where(lane < qi, gate, -jnp.inf)
        lane_f = lane.astype(F32)
        notsel = jnp.ones((rows, LANES), F32)
        for _ in range(MOBA_TOPK):
            mx = jnp.max(gate, axis=1, keepdims=True)
            idx = jnp.min(jnp.where(gate == mx, lane_f, float(LANES)), axis=1, keepdims=True)
            hit = lane_f == idx
            notsel = jnp.where(hit & (mx > -jnp.inf), 0.0, notsel)
            gate = jnp.where(hit, -jnp.inf, gate)
        aug_lhs = notsel.astype(BF16)
    else:
        mine = ((row < tq) & (lane < 3)) | ((row >= tq) & (lane >= 3) & (lane < 6))
        aug_lhs = jnp.where(mine, 1.0, 0.0).astype(BF16)
    lhs_ref[...] = jnp.concatenate([qs, aug_lhs], axis=1)

    m_ref[...] = jnp.full_like(m_ref, NEG_INIT)
    l_ref[...] = jnp.zeros_like(l_ref)
    acc_ref[...] = jnp.zeros_like(acc_ref)

    def block(start, diagonal):
        kb = k_ref[pl.ds(start, tk), :]
        if diagonal and moba:
            ab = jnp.zeros((tk, LANES), BF16)
        else:
            ab = aug_ref[pl.ds(start, tk), :]
        rhs = jnp.concatenate([kb, ab], axis=1)
        s = lax.dot_general(lhs_ref[...], rhs, (((1,), (1,)), ((), ())),
                            preferred_element_type=F32)
        if diagonal:
            r = lax.broadcasted_iota(jnp.int32, (rows, tk), 0)
            c = lax.broadcasted_iota(jnp.int32, (rows, tk), 1)
            s = jnp.where(c <= jnp.where(r < tq, r, r - tq), s, NEG_INIT)
        m_prev = m_ref[...]
        m_new = jnp.maximum(m_prev, jnp.max(s, axis=1, keepdims=True))
        alpha = jnp.exp(m_prev - m_new)
        p = jnp.exp(s - m_new)
        l_ref[...] = alpha * l_ref[...] + jnp.sum(p, axis=1, keepdims=True)
        acc_ref[...] = alpha * acc_ref[...] + jnp.dot(
            p.astype(BF16), v_ref[pl.ds(start, tk), :], preferred_element_type=F32)
        m_ref[...] = m_new

    def body(j, carry):
        block(pl.multiple_of(j * tk, tk), False)
        return carry

    lax.fori_loop(0, qi, body, 0)
    block(pl.multiple_of(qi * tk, tk), True)

    out = acc_ref[...] / l_ref[...]
    o_ref[...] = jnp.where(lane_q < HEAD_DIM, out[:tq], out[tq:]).astype(o_ref.dtype)


def _attention(qkv, aug, kmean, *, B, S, col_q, moba):
    T = qkv.shape[0]
    tq = MOBA_BLOCK
    nq = S // tq
    q_spec = pl.BlockSpec((tq, LANES), lambda b, p, i: (b * nq + i, col_q + p))
    k_spec = pl.BlockSpec((S, LANES), lambda b, p, i: (b, col_q + PAIRS + p))
    v_spec = pl.BlockSpec((S, LANES), lambda b, p, i: (b, col_q + 2 * PAIRS + p))
    if moba:
        aug_spec = pl.BlockSpec((S, LANES), lambda b, p, i: (0, 0))
        extra = [pl.BlockSpec((S // MOBA_BLOCK, LANES), lambda b, p, i: (b, p))]
        args = (qkv, qkv, qkv, aug, kmean)
    else:
        aug_spec = pl.BlockSpec((S, LANES), lambda b, p, i: (b, p))
        extra = []
        args = (qkv, qkv, qkv, aug)
    return pl.pallas_call(
        functools.partial(_attn_kernel, moba=moba, tq=tq),
        grid=(B, PAIRS, nq),
        in_specs=[q_spec, k_spec, v_spec, aug_spec] + extra,
        out_specs=pl.BlockSpec((tq, LANES), lambda b, p, i: (b * nq + i, p)),
        out_shape=jax.ShapeDtypeStruct((T, WIDTH), BF16),
        scratch_shapes=[pltpu.VMEM((2 * tq, 2 * LANES), BF16), pltpu.VMEM((2 * tq, 1), F32),
                        pltpu.VMEM((2 * tq, 1), F32), pltpu.VMEM((2 * tq, LANES), F32)],
        compiler_params=_params(("arbitrary", "arbitrary", "arbitrary")),
        name="moba_attn" if moba else "fox_attn",
    )(*args)


def _moba_block_mask(S):
    nb = S // MOBA_BLOCK
    e = np.zeros((S, LANES), np.float32)
    e[np.arange(S), np.arange(S) // MOBA_BLOCK] = -MASK_BIG
    assert nb <= LANES
    return jnp.asarray(e, BF16)


def _merge_kernel(ya_ref, yb_ref, ga_ref, gb_ref, x_ref, wa_ref, wb_ref, wo_ref, g_ref,
                  h_ref, xnt_ref):
    ya = jnp.dot(ya_ref[...], wa_ref[...], preferred_element_type=F32)
    yb = jnp.dot(yb_ref[...], wb_ref[...], preferred_element_type=F32)
    merged = ga_ref[...] * ya + gb_ref[...] * yb
    h = x_ref[...] + jnp.dot(merged.astype(BF16), wo_ref[...], preferred_element_type=F32)
    h_ref[...] = h
    hn = h * lax.rsqrt(jnp.mean(h * h, axis=-1, keepdims=True) + RMS_EPS) * g_ref[...]
    xnt_ref[...] = hn.T.astype(BF16)


def _merge(attn_a, attn_b, gates, xt, wa, wb, wo, g):
    T, D = xt.shape
    tm = 512
    const = lambda *shape: pl.BlockSpec(shape, lambda i: (0,) * len(shape))
    return pl.pallas_call(
        _merge_kernel,
        grid=(T // tm,),
        in_specs=[pl.BlockSpec((tm, WIDTH), lambda i: (i, 0)),
                  pl.BlockSpec((tm, WIDTH), lambda i: (i, 0)),
                  pl.BlockSpec((tm, D), lambda i: (i, 0)),
                  pl.BlockSpec((tm, D), lambda i: (i, 1)),
                  pl.BlockSpec((tm, D), lambda i: (i, 0)),
                  const(WIDTH, D), const(WIDTH, D), const(D, D), const(1, D)],
        out_specs=[pl.BlockSpec((tm, D), lambda i: (i, 0)),
                   pl.BlockSpec((D, tm), lambda i: (0, i))],
        out_shape=[jax.ShapeDtypeStruct((T, D), F32), jax.ShapeDtypeStruct((D, T), BF16)],
        compiler_params=_params(("arbitrary",)),
        name="merge",
    )(attn_a, attn_b, gates, gates, xt, wa.astype(BF16), wb.astype(BF16), wo.astype(BF16),
      g.reshape(1, D))


_N_SORTED = PEER_TOPK + 1
_CAND = [(a, b) for a in range(_N_SORTED) for b in range(_N_SORTED)
         if (a + 1) * (b + 1) <= _N_SORTED]
_CAND_ROWS = -(-len(_CAND) // 8) * 8


def _top_rows(x, k):
    vals = []
    for _ in range(k):
        m = jnp.max(x, axis=0, keepdims=True)
        vals.append(m)
        x = jnp.where(x == m, -jnp.inf, x)
    return vals


def _route_kernel(xt_ref, wq_ref, keys_ref, thr_ref, e1_ref, s2_ref, e2_ref, cand_ref):
    qt = jnp.dot(wq_ref[...], xt_ref[...], preferred_element_type=F32).astype(BF16)
    for h in range(PEER_HEADS):
        s1 = jnp.dot(keys_ref[2 * h], qt[(2 * h) * PEER_HALF:(2 * h + 1) * PEER_HALF, :],
                     preferred_element_type=F32)
        s2 = jnp.dot(keys_ref[2 * h + 1], qt[(2 * h + 1) * PEER_HALF:(2 * h + 2) * PEER_HALF, :],
                     preferred_element_type=F32)
        sv1 = _top_rows(s1, _N_SORTED)
        sv2 = _top_rows(s2, _N_SORTED)
        cand_ref[...] = jnp.full(cand_ref.shape, -jnp.inf, F32)
        for r, (a, b) in enumerate(_CAND):
            cand_ref[r:r + 1, :] = sv1[a] + sv2[b]
        cv = _top_rows(cand_ref[...], _N_SORTED)
        thr = 0.5 * (cv[PEER_TOPK - 1] + cv[PEER_TOPK])
        z = sum(jnp.exp(v - cv[0]) for v in cv[:PEER_TOPK])
        thr_ref[h] = thr - s1
        e1_ref[h] = jnp.exp(s1 - sv1[0]) / z
        s2_ref[h] = s2
        e2_ref[h] = jnp.exp(s2 - sv2[0])


def _route(xnt, w_peer_q, sub_keys):
    D, T = xnt.shape
    tm = 256
    nq = PEER_HEADS * 2 * PEER_HALF
    wqt = w_peer_q.T.astype(BF16)
    keys = sub_keys.reshape(PEER_HEADS * 2, PEER_NKEYS, PEER_HALF).astype(BF16)
    out_spec = pl.BlockSpec((PEER_HEADS, PEER_NKEYS, tm), lambda i: (0, 0, i))
    out_shape = jax.ShapeDtypeStruct((PEER_HEADS, PEER_NKEYS, T), F32)
    return pl.pallas_call(
        _route_kernel,
        grid=(T // tm,),
        in_specs=[pl.BlockSpec((D, tm), lambda i: (0, i)),
                  pl.BlockSpec((nq, D), lambda i: (0, 0)),
                  pl.BlockSpec((PEER_HEADS * 2, PEER_NKEYS, PEER_HALF), lambda i: (0, 0, 0))],
        out_specs=[out_spec] * 4,
        out_shape=[out_shape] * 4,
        scratch_shapes=[pltpu.VMEM((_CAND_ROWS, tm), F32)],
        compiler_params=_params(("arbitrary",)),
        name="route",
    )(xnt, wqt, keys)


def _peer_kernel(thr_ref, e1_ref, s2_ref, e2_ref, xt_ref, u_ref, vt_ref, h_ref, g_ref,
                 o_ref, yt_ref, a_ref, *, ti):
    e = pl.program_id(1)

    @pl.when(e == 0)
    def _():
        yt_ref[...] = jnp.zeros_like(yt_ref)

    hval = jnp.dot(u_ref[...], xt_ref[...], preferred_element_type=F32)
    for il in range(ti):
        ig = e * ti + il
        w = jnp.zeros((PEER_NKEYS, hval.shape[1]), F32)
        for h in range(PEER_HEADS):
            sel = s2_ref[h] >= thr_ref[h, pl.ds(ig, 1), :]
            w = w + jnp.where(sel, e2_ref[h] * e1_ref[h, pl.ds(ig, 1), :], 0.0)
        hv = hval[il * PEER_NKEYS:(il + 1) * PEER_NKEYS, :]
        a_ref[il * PEER_NKEYS:(il + 1) * PEER_NKEYS, :] = (
            w * (0.5 * hv * (1.0 + lax.erf(hv * SQRT_HALF)))).astype(BF16)
    yt_ref[...] += jnp.dot(vt_ref[...], a_ref[...], preferred_element_type=F32)

    @pl.when(e == pl.num_programs(1) - 1)
    def _():
        hh = h_ref[...] + yt_ref[...].T
        o_ref[...] = hh * lax.rsqrt(jnp.mean(hh * hh, axis=-1, keepdims=True) + RMS_EPS) * g_ref[...]


def _peer(route, xnt, expert_u, expert_v, h1, g):
    thr, e1, s2, e2 = route
    D, T = xnt.shape
    tm = 512
    ti = 8
    te = ti * PEER_NKEYS
    u = expert_u.astype(BF16)
    vt = expert_v.T.astype(BF16)
    r_spec = pl.BlockSpec((PEER_HEADS, PEER_NKEYS, tm), lambda t, e: (0, 0, t))
    return pl.pallas_call(
        functools.partial(_peer_kernel, ti=ti),
        grid=(T // tm, PEER_NKEYS // ti),
        in_specs=[r_spec, r_spec, r_spec, r_spec,
                  pl.BlockSpec((D, tm), lambda t, e: (0, t)),
                  pl.BlockSpec((te, D), lambda t, e: (e, 0)),
                  pl.BlockSpec((D, te), lambda t, e: (0, e)),
                  pl.BlockSpec((tm, D), lambda t, e: (t, 0)),
                  pl.BlockSpec((1, D), lambda t, e: (0, 0))],
        out_specs=pl.BlockSpec((tm, D), lambda t, e: (t, 0)),
        out_shape=jax.ShapeDtypeStruct((T, D), F32),
        scratch_shapes=[pltpu.VMEM((D, tm), F32), pltpu.VMEM((te, tm), BF16)],
        compiler_params=_params(("arbitrary", "arbitrary")),
        name="peer",
    )(thr, e1, s2, e2, xnt, u, vt, h1, g.reshape(1, D))


def kernel(x, norm_mix_g, w_in, b_forget, w_branch_a, w_branch_b, w_out, norm_ffn_g, w_peer_q,
           peer_sub_keys, peer_expert_u, peer_expert_v, norm_final_g):
    B, S, D = x.shape
    depth = w_in.shape[0]
    assert depth == 1, "the final RMSNorm is fused into the last PEER call of a single layer"
    assert S % MOBA_BLOCK == 0 and D == 2 * WIDTH
    h = x.reshape(B * S, D)
    qkv, gates, kmean, caug = _inproj(h, norm_mix_g[0], w_in[0], b_forget[0], S)
    attn_a = _attention(qkv, _moba_block_mask(S), kmean.reshape(-1, WIDTH),
                        B=B, S=S, col_q=0, moba=True)
    attn_b = _attention(qkv, caug, None, B=B, S=S, col_q=3 * PAIRS, moba=False)
    h1, xnt = _merge(attn_a, attn_b, gates, h, w_branch_a[0], w_branch_b[0], w_out[0],
                     norm_ffn_g[0])
    route = _route(xnt, w_peer_q[0], peer_sub_keys[0])
    out = _peer(route, xnt, peer_expert_u[0], peer_expert_v[0], h1, norm_final_g)
    return out.reshape(B, S, D)
```

```python
import functools

import numpy as np
import jax
import jax.numpy as jnp
from jax import lax
from jax.experimental import pallas as pl
from jax.experimental.pallas import tpu as pltpu

F32 = jnp.float32
BF16 = jnp.bfloat16

LANES = 128

HEAD_DIM = 64
HEADS = 8
WIDTH = HEADS * HEAD_DIM
PAIRS = WIDTH // LANES
ROPE_DIM = HEAD_DIM // 4
ROPE_HALF = ROPE_DIM // 2
ROPE_THETA = 500000.0
MOBA_BLOCK = 256
MOBA_TOPK = 3
RMS_EPS = 1e-6
PEER_HEADS = 8
PEER_NKEYS = 128
PEER_HALF = 128
PEER_TOPK = 16
PEER_JB = 64
PEER_IG = 4
ATTN_SCALE = HEAD_DIM ** -0.5

MASK_BIG = 2.0 ** 100
SQRT_HALF = 0.5 ** 0.5
NEG_INIT = -1e30

VMEM_LIMIT = 56 * 1024 * 1024


def _params(sem, flags=None):
    return pltpu.CompilerParams(dimension_semantics=sem, vmem_limit_bytes=VMEM_LIMIT, flags=flags)


def _split3(v):
    hi = v.astype(BF16)
    r = v - hi.astype(F32)
    mid = r.astype(BF16)
    lo = (r - mid.astype(F32)).astype(BF16)
    return hi, mid, lo


def _inproj_kernel(x_ref, g_ref, wqkv_ref, wg_ref, wf_ref, bf_ref, cos_ref, sinp_ref, sinm_ref,
                   tri_ref, place_ref, qkv_ref, gate_ref, kmean_ref, caug_ref, carry_ref,
                   *, tiles_per_seq):
    i = pl.program_id(0)
    tm = x_ref.shape[0]
    x = x_ref[...]
    xn = x * lax.rsqrt(jnp.mean(x * x, axis=-1, keepdims=True) + RMS_EPS) * g_ref[...]
    xb = xn.astype(BF16)

    qkv = jnp.dot(xb, wqkv_ref[...], preferred_element_type=F32)
    cos, sinp, sinm = cos_ref[...], sinp_ref[...], sinm_ref[...]

    def rope(c0):
        t = qkv[:, c0:c0 + LANES]
        return (t * cos + pltpu.roll(t, ROPE_HALF, 1) * sinp
                + pltpu.roll(t, LANES - ROPE_HALF, 1) * sinm)

    ksum = []
    for c in range(PAIRS):
        qa = rope(c * LANES) * ATTN_SCALE
        ka = rope(WIDTH + c * LANES)
        qkv_ref[:, c * LANES:(c + 1) * LANES] = qa.astype(BF16)
        qkv_ref[:, WIDTH + c * LANES:WIDTH + (c + 1) * LANES] = ka.astype(BF16)
        ksum.append(jnp.mean(ka, axis=0, keepdims=True))
    kmean_ref[...] = jnp.concatenate(ksum, axis=1)
    qkv_ref[:, 2 * WIDTH:3 * WIDTH] = qkv[:, 2 * WIDTH:3 * WIDTH].astype(BF16)
    qkv_ref[:, 3 * WIDTH:4 * WIDTH] = (qkv[:, 3 * WIDTH:4 * WIDTH] * ATTN_SCALE).astype(BF16)
    qkv_ref[:, 4 * WIDTH:6 * WIDTH] = qkv[:, 4 * WIDTH:6 * WIDTH].astype(BF16)

    gate_ref[...] = jax.nn.sigmoid(jnp.dot(xb, wg_ref[...], preferred_element_type=F32))

    fl = jnp.dot(xb, wf_ref[...], preferred_element_type=F32) + bf_ref[...]
    logf = jnp.minimum(fl, 0.0) - jnp.log1p(jnp.exp(-jnp.abs(fl)))
    tri = tri_ref[...]
    cum = sum(jnp.dot(tri, part, preferred_element_type=F32) for part in _split3(logf))

    @pl.when(i % tiles_per_seq == 0)
    def _():
        carry_ref[...] = jnp.zeros_like(carry_ref)

    c = cum + carry_ref[...]
    carry_ref[...] = c[tm - 1:tm, :]
    aug = sum(jnp.dot(part, place_ref[j], preferred_element_type=F32)
              for j, part in enumerate(_split3(-c)))
    caug_ref[...] = aug.astype(BF16)


def _rope_tables(S):
    half = ROPE_HALF
    inv_freq = jnp.power(ROPE_THETA, -jnp.arange(half, dtype=F32) / half)
    ang = jnp.arange(S).astype(F32)[:, None] * inv_freq[None, :]
    cos, sin = jnp.cos(ang), jnp.sin(ang)
    ones = jnp.ones((S, HEAD_DIM - ROPE_DIM), F32)
    zeros = jnp.zeros((S, HEAD_DIM - ROPE_DIM), F32)
    zh = jnp.zeros((S, half), F32)
    cos_h = jnp.concatenate([cos, cos, ones], axis=1)
    sinp_h = jnp.concatenate([zh, sin, zeros], axis=1)
    sinm_h = jnp.concatenate([-sin, zh, zeros], axis=1)
    rep = LANES // HEAD_DIM
    return tuple(jnp.tile(t, (1, rep)) for t in (cos_h, sinp_h, sinm_h))


def _place_matrices():
    p = np.zeros((3, LANES, WIDTH), np.float32)
    for h in range(HEADS):
        for j in range(3):
            p[j, h, (h // 2) * LANES + 3 * (h % 2) + j] = 1.0
    return jnp.asarray(p, BF16)


def _inproj(xt, g, w_in, b_forget, S):
    T, D = xt.shape
    tm = MOBA_BLOCK
    n_qkv = 6 * WIDTH
    wqkv = w_in[:, :n_qkv].astype(BF16)
    wf = jnp.pad(w_in[:, n_qkv:n_qkv + HEADS], ((0, 0), (0, LANES - HEADS))).astype(BF16)
    wg = w_in[:, n_qkv + HEADS:].astype(BF16)
    bf = jnp.pad(b_forget, (0, LANES - HEADS)).reshape(1, LANES).astype(F32)
    cos, sinp, sinm = _rope_tables(S)
    tri = jnp.asarray(np.tril(np.ones((tm, tm), np.float32)), BF16)
    place = _place_matrices()
    tiles_per_seq = S // tm
    const = lambda *shape: pl.BlockSpec(shape, lambda i: (0,) * len(shape))
    pos = pl.BlockSpec((tm, LANES), lambda i: (i % tiles_per_seq, 0))
    return pl.pallas_call(
        functools.partial(_inproj_kernel, tiles_per_seq=tiles_per_seq),
        grid=(T // tm,),
        in_specs=[pl.BlockSpec((tm, D), lambda i: (i, 0)), const(1, D), const(D, n_qkv),
                  const(D, 2 * D), const(D, LANES), const(1, LANES), pos, pos, pos,
                  const(tm, tm), const(3, LANES, WIDTH)],
        out_specs=[pl.BlockSpec((tm, n_qkv), lambda i: (i, 0)),
                   pl.BlockSpec((tm, 2 * D), lambda i: (i, 0)),
                   pl.BlockSpec((None, 1, WIDTH), lambda i: (i, 0, 0)),
                   pl.BlockSpec((tm, WIDTH), lambda i: (i, 0))],
        out_shape=[jax.ShapeDtypeStruct((T, n_qkv), BF16),
                   jax.ShapeDtypeStruct((T, 2 * D), F32),
                   jax.ShapeDtypeStruct((T // tm, 1, WIDTH), F32),
                   jax.ShapeDtypeStruct((T, WIDTH), BF16)],
        scratch_shapes=[pltpu.VMEM((1, LANES), F32)],
        compiler_params=_params(("arbitrary",)),
        name="inproj",
    )(xt, g.reshape(1, D), wqkv, wg, wf, bf, cos, sinp, sinm, tri, place)


def _attn_kernel(*refs, moba, tq):
    if moba:
        q_ref, k_ref, v_ref, aug_ref, kmean_ref, o_ref, lhs_ref, m_ref, l_ref, acc_ref = refs
    else:
        q_ref, k_ref, v_ref, aug_ref, o_ref, lhs_ref, m_ref, l_ref, acc_ref = refs
    qi = pl.program_id(2)
    tk = tq
    rows = 2 * tq
    q = q_ref[...]
    lane_q = lax.broadcasted_iota(jnp.int32, (tq, LANES), 1)
    zq = jnp.zeros_like(q)
    qs = jnp.concatenate([jnp.where(lane_q < HEAD_DIM, q, zq),
                          jnp.where(lane_q < HEAD_DIM, zq, q)], axis=0)
    row = lax.broadcasted_iota(jnp.int32, (rows, LANES), 0)
    lane = lax.broadcasted_iota(jnp.int32, (rows, LANES), 1)

    if moba:
        nb = kmean_ref.shape[0]
        km = jnp.concatenate([kmean_ref[...].astype(BF16),
                              jnp.zeros((LANES - nb, LANES), BF16)], axis=0)
        gate = lax.dot_general(qs, km, (((1,), (1,)), ((), ())), preferred_element_type=F32)
        gate = jnp.where(lane < qi, gate, -jnp.inf)
        lane_f = lane.astype(F32)
        notsel = jnp.ones((rows, LANES), F32)
        for _ in range(MOBA_TOPK):
            mx = jnp.max(gate, axis=1, keepdims=True)
            idx = jnp.min(jnp.where(gate == mx, lane_f, float(LANES)), axis=1, keepdims=True)
            hit = lane_f == idx
            notsel = jnp.where(hit & (mx > -jnp.inf), 0.0, notsel)
            gate = jnp.where(hit, -jnp.inf, gate)
        aug_lhs = notsel.astype(BF16)
    else:
        mine = ((row < tq) & (lane < 3)) | ((row >= tq) & (lane >= 3) & (lane < 6))
        aug_lhs = jnp.where(mine, 1.0, 0.0).astype(BF16)
    lhs_ref[...] = jnp.concatenate([qs, aug_lhs], axis=1)

    m_ref[...] = jnp.full_like(m_ref, NEG_INIT)
    l_ref[...] = jnp.zeros_like(l_ref)
    acc_ref[...] = jnp.zeros_like(acc_ref)

    def block(start, width, diagonal):
        kb = k_ref[pl.ds(start, width), :]
        if diagonal and moba:
            ab = jnp.zeros((width, LANES), BF16)
        else:
            ab = aug_ref[pl.ds(start, width), :]
        rhs = jnp.concatenate([kb, ab], axis=1)
        s = lax.dot_general(lhs_ref[...], rhs, (((1,), (1,)), ((), ())),
                            preferred_element_type=F32)
        if diagonal:
            r = lax.broadcasted_iota(jnp.int32, (rows, width), 0)
            c = lax.broadcasted_iota(jnp.int32, (rows, width), 1)
            s = jnp.where(c <= jnp.where(r < tq, r, r - tq), s, NEG_INIT)
        m_prev = m_ref[...]
        m_new = jnp.maximum(m_prev, jnp.max(s, axis=1, keepdims=True))
        alpha = jnp.exp(m_prev - m_new)
        ps = [jnp.exp(s[:, c0:c0 + LANES] - m_new) for c0 in range(0, width, LANES)]
        l_ref[...] = alpha * l_ref[...] + sum(ps)
        p = jnp.concatenate(ps, axis=1).astype(BF16)
        acc_ref[...] = alpha * acc_ref[...] + jnp.dot(
            p, v_ref[pl.ds(start, width), :], preferred_element_type=F32)
        m_ref[...] = m_new

    wide = 4 * tk

    def body(j, carry):
        block(pl.multiple_of(j * wide, wide), wide, False)
        return carry

    n_wide = qi // 4
    lax.fori_loop(0, n_wide, body, 0)
    rem = qi - 4 * n_wide

    @pl.when(rem >= 2)
    def _():
        block(pl.multiple_of(n_wide * wide, 2 * tk), 2 * tk, False)

    @pl.when(rem % 2 == 1)
    def _():
        block(pl.multiple_of((qi - 1) * tk, tk), tk, False)

    block(pl.multiple_of(qi * tk, tk), tk, True)

    out = acc_ref[...] / jnp.sum(l_ref[...], axis=1, keepdims=True)
    o_ref[...] = jnp.where(lane_q < HEAD_DIM, out[:tq], out[tq:]).astype(o_ref.dtype)


def _attention(qkv, aug, kmean, *, B, S, col_q, moba):
    T = qkv.shape[0]
    tq = MOBA_BLOCK
    nq = S // tq
    q_spec = pl.BlockSpec((tq, LANES), lambda b, p, i: (b * nq + i, col_q + p))
    k_spec = pl.BlockSpec((S, LANES), lambda b, p, i: (b, col_q + PAIRS + p))
    v_spec = pl.BlockSpec((S, LANES), lambda b, p, i: (b, col_q + 2 * PAIRS + p))
    if moba:
        aug_spec = pl.BlockSpec((S, LANES), lambda b, p, i: (0, 0))
        extra = [pl.BlockSpec((S // MOBA_BLOCK, LANES), lambda b, p, i: (b, p))]
        args = (qkv, qkv, qkv, aug, kmean)
    else:
        aug_spec = pl.BlockSpec((S, LANES), lambda b, p, i: (b, p))
        extra = []
        args = (qkv, qkv, qkv, aug)
    return pl.pallas_call(
        functools.partial(_attn_kernel, moba=moba, tq=tq),
        grid=(B, PAIRS, nq),
        in_specs=[q_spec, k_spec, v_spec, aug_spec] + extra,
        out_specs=pl.BlockSpec((tq, LANES), lambda b, p, i: (b * nq + i, p)),
        out_shape=jax.ShapeDtypeStruct((T, WIDTH), BF16),
        scratch_shapes=[pltpu.VMEM((2 * tq, 2 * LANES), BF16), pltpu.VMEM((2 * tq, LANES), F32),
                        pltpu.VMEM((2 * tq, LANES), F32), pltpu.VMEM((2 * tq, LANES), F32)],
        compiler_params=_params(("arbitrary", "arbitrary", "arbitrary")),
        name="moba_attn" if moba else "fox_attn",
    )(*args)


def _moba_block_mask(S):
    nb = S // MOBA_BLOCK
    e = np.zeros((S, LANES), np.float32)
    e[np.arange(S), np.arange(S) // MOBA_BLOCK] = -MASK_BIG
    assert nb <= LANES
    return jnp.asarray(e, BF16)


def _merge_kernel(ya_ref, yb_ref, ga_ref, gb_ref, x_ref, wa_ref, wb_ref, wo_ref, g_ref,
                  h_ref, xnt_ref):
    ya = jnp.dot(ya_ref[...], wa_ref[...], preferred_element_type=F32)
    yb = jnp.dot(yb_ref[...], wb_ref[...], preferred_element_type=F32)
    merged = ga_ref[...] * ya + gb_ref[...] * yb
    h = x_ref[...] + jnp.dot(merged.astype(BF16), wo_ref[...], preferred_element_type=F32)
    h_ref[...] = h
    hn = h * lax.rsqrt(jnp.mean(h * h, axis=-1, keepdims=True) + RMS_EPS) * g_ref[...]
    xnt_ref[...] = hn.T.astype(BF16)


def _merge(attn_a, attn_b, gates, xt, wa, wb, wo, g):
    T, D = xt.shape
    tm = 512
    const = lambda *shape: pl.BlockSpec(shape, lambda i: (0,) * len(shape))
    return pl.pallas_call(
        _merge_kernel,
        grid=(T // tm,),
        in_specs=[pl.BlockSpec((tm, WIDTH), lambda i: (i, 0)),
                  pl.BlockSpec((tm, WIDTH), lambda i: (i, 0)),
                  pl.BlockSpec((tm, D), lambda i: (i, 0)),
                  pl.BlockSpec((tm, D), lambda i: (i, 1)),
                  pl.BlockSpec((tm, D), lambda i: (i, 0)),
                  const(WIDTH, D), const(WIDTH, D), const(D, D), const(1, D)],
        out_specs=[pl.BlockSpec((tm, D), lambda i: (i, 0)),
                   pl.BlockSpec((D, tm), lambda i: (0, i))],
        out_shape=[jax.ShapeDtypeStruct((T, D), F32), jax.ShapeDtypeStruct((D, T), BF16)],
        compiler_params=_params(("arbitrary",)),
        name="merge",
    )(attn_a, attn_b, gates, gates, xt, wa.astype(BF16), wb.astype(BF16), wo.astype(BF16),
      g.reshape(1, D))


_N_SORTED = PEER_TOPK + 1
_CAND = [(a, b) for a in range(_N_SORTED) for b in range(_N_SORTED)
         if (a + 1) * (b + 1) <= _N_SORTED]
_CAND_ROWS = -(-len(_CAND) // 8) * 8


def _top_rows(x, k):
    vals = []
    for _ in range(k):
        m = jnp.max(x, axis=0, keepdims=True)
        vals.append(m)
        x = jnp.where(x == m, -jnp.inf, x)
    return vals


def _route_kernel(xt_ref, wq_ref, keys_ref, thr_ref, e1_ref, s2_ref, e2_ref, cand_ref):
    qt = jnp.dot(wq_ref[...], xt_ref[...], preferred_element_type=F32).astype(BF16)
    for h in range(PEER_HEADS):
        s1 = jnp.dot(keys_ref[2 * h], qt[(2 * h) * PEER_HALF:(2 * h + 1) * PEER_HALF, :],
                     preferred_element_type=F32)
        s2 = jnp.dot(keys_ref[2 * h + 1], qt[(2 * h + 1) * PEER_HALF:(2 * h + 2) * PEER_HALF, :],
                     preferred_element_type=F32)
        sv1 = _top_rows(s1, _N_SORTED)
        sv2 = _top_rows(s2, _N_SORTED)
        cand_ref[...] = jnp.full(cand_ref.shape, -jnp.inf, F32)
        for r, (a, b) in enumerate(_CAND):
            cand_ref[r:r + 1, :] = sv1[a] + sv2[b]
        cv = _top_rows(cand_ref[...], _N_SORTED)
        thr = 0.5 * (cv[PEER_TOPK - 1] + cv[PEER_TOPK])
        z = sum(jnp.exp(v - cv[0]) for v in cv[:PEER_TOPK])
        thr_ref[h] = thr - s1
        e1_ref[h] = jnp.exp(s1 - sv1[0]) / z
        s2_ref[h] = s2
        e2_ref[h] = jnp.exp(s2 - sv2[0])


def _route(xnt, w_peer_q, sub_keys):
    D, T = xnt.shape
    tm = 256
    nq = PEER_HEADS * 2 * PEER_HALF
    wqt = w_peer_q.T.astype(BF16)
    keys = sub_keys.reshape(PEER_HEADS * 2, PEER_NKEYS, PEER_HALF).astype(BF16)
    out_spec = pl.BlockSpec((PEER_HEADS, PEER_NKEYS, tm), lambda i: (0, 0, i))
    out_shape = jax.ShapeDtypeStruct((PEER_HEADS, PEER_NKEYS, T), F32)
    return pl.pallas_call(
        _route_kernel,
        grid=(T // tm,),
        in_specs=[pl.BlockSpec((D, tm), lambda i: (0, i)),
                  pl.BlockSpec((nq, D), lambda i: (0, 0)),
                  pl.BlockSpec((PEER_HEADS * 2, PEER_NKEYS, PEER_HALF), lambda i: (0, 0, 0))],
        out_specs=[out_spec] * 4,
        out_shape=[out_shape] * 4,
        scratch_shapes=[pltpu.VMEM((_CAND_ROWS, tm), F32)],
        compiler_params=_params(("arbitrary",)),
        name="route",
    )(xnt, wqt, keys)


def _peer_kernel(thr_ref, e1_ref, s2_ref, e2_ref, xt_ref, u_ref, vt_ref, h_ref, g_ref,
                 o_ref, yt_ref, hv_ref, a_ref, *, ti):
    e = pl.program_id(1)
    tm = xt_ref.shape[1]

    @pl.when(e == 0)
    def _():
        yt_ref[...] = jnp.zeros_like(yt_ref)

    hv_ref[...] = jnp.dot(u_ref[...], xt_ref[...], preferred_element_type=F32)
    i0 = pl.multiple_of(e * ti, ti)
    for g0 in range(0, ti, PEER_IG):
        for c0 in range(0, tm, LANES):
            cs = slice(c0, c0 + LANES)
            for j0 in range(0, PEER_NKEYS, PEER_JB):
                js = slice(j0, j0 + PEER_JB)
                w = [jnp.zeros((PEER_JB, LANES), F32) for _ in range(PEER_IG)]
                for h in range(PEER_HEADS):
                    s2 = s2_ref[h, js, cs]
                    e2 = e2_ref[h, js, cs]
                    thr = thr_ref[h, pl.ds(i0, ti), cs]
                    e1 = e1_ref[h, pl.ds(i0, ti), cs]
                    for k in range(PEER_IG):
                        i = g0 + k
                        sel = s2 >= thr[i:i + 1, :]
                        w[k] = w[k] + jnp.where(sel, e2 * e1[i:i + 1, :], 0.0)
                for k in range(PEER_IG):
                    r0 = (g0 + k) * PEER_NKEYS + j0
                    hv = hv_ref[r0:r0 + PEER_JB, cs]
                    gelu = 0.5 * hv * (1.0 + lax.erf(hv * SQRT_HALF))
                    a_ref[r0:r0 + PEER_JB, cs] = (w[k] * gelu).astype(BF16)
        rows = slice(g0 * PEER_NKEYS, (g0 + PEER_IG) * PEER_NKEYS)
        yt_ref[...] += jnp.dot(vt_ref[:, rows], a_ref[rows, :], preferred_element_type=F32)

    @pl.when(e == pl.num_programs(1) - 1)
    def _():
        hh = h_ref[...] + yt_ref[...].T
        o_ref[...] = hh * lax.rsqrt(jnp.mean(hh * hh, axis=-1, keepdims=True) + RMS_EPS) * g_ref[...]


def _peer(route, xnt, expert_u, expert_v, h1, g):
    thr, e1, s2, e2 = route
    D, T = xnt.shape
    tm = 512
    ti = 8
    te = ti * PEER_NKEYS
    u = expert_u.astype(BF16)
    vt = expert_v.T.astype(BF16)
    r_spec = pl.BlockSpec((PEER_HEADS, PEER_NKEYS, tm), lambda t, e: (0, 0, t))
    return pl.pallas_call(
        functools.partial(_peer_kernel, ti=ti),
        grid=(T // tm, PEER_NKEYS // ti),
        in_specs=[r_spec, r_spec, r_spec, r_spec,
                  pl.BlockSpec((D, tm), lambda t, e: (0, t)),
                  pl.BlockSpec((te, D), lambda t, e: (e, 0)),
                  pl.BlockSpec((D, te), lambda t, e: (0, e)),
                  pl.BlockSpec((tm, D), lambda t, e: (t, 0)),
                  pl.BlockSpec((1, D), lambda t, e: (0, 0))],
        out_specs=pl.BlockSpec((tm, D), lambda t, e: (t, 0)),
        out_shape=jax.ShapeDtypeStruct((T, D), F32),
        scratch_shapes=[pltpu.VMEM((D, tm), F32), pltpu.VMEM((te, tm), F32),
                        pltpu.VMEM((te, tm), BF16)],
        compiler_params=_params(("arbitrary", "arbitrary")),
        name="peer",
    )(thr, e1, s2, e2, xnt, u, vt, h1, g.reshape(1, D))


def kernel(x, norm_mix_g, w_in, b_forget, w_branch_a, w_branch_b, w_out, norm_ffn_g, w_peer_q,
           peer_sub_keys, peer_expert_u, peer_expert_v, norm_final_g):
    B, S, D = x.shape
    depth = w_in.shape[0]
    assert depth == 1, "the final RMSNorm is fused into the last PEER call of a single layer"
    assert S % MOBA_BLOCK == 0 and D == 2 * WIDTH
    h = x.reshape(B * S, D)
    qkv, gates, kmean, caug = _inproj(h, norm_mix_g[0], w_in[0], b_forget[0], S)
    attn_a = _attention(qkv, _moba_block_mask(S), kmean.reshape(-1, WIDTH),
                        B=B, S=S, col_q=0, moba=True)
    attn_b = _attention(qkv, caug, None, B=B, S=S, col_q=3 * PAIRS, moba=False)
    h1, xnt = _merge(attn_a, attn_b, gates, h, w_branch_a[0], w_branch_b[0], w_out[0],
                     norm_ffn_g[0])
    route = _route(xnt, w_peer_q[0], peer_sub_keys[0])
    out = _peer(route, xnt, peer_expert_u[0], peer_expert_v[0], h1, norm_final_g)
    return out.reshape(B, S, D)
```
